```python
import math
import jax, jax.numpy as jnp
from jax import lax
import numpy as np

D_MODEL = 1024
BATCH = 4
SEQ = 4096
DEPTH = 1
DEC_BATCH = 32
DEC_SEQ = 2048
PAST_LEN = 128

MIX_WIDTH = D_MODEL
RET_WIDTH = MIX_WIDTH // 2
ATT_WIDTH = MIX_WIDTH - RET_WIDTH
RET_HEADS = 4
RET_HEAD_DIM = RET_WIDTH // RET_HEADS
RET_CHUNK = 128
ATT_Q_HEADS = 8
ATT_KV_HEADS = 2
ATT_GROUP = ATT_Q_HEADS // ATT_KV_HEADS
ATT_HEAD_DIM = ATT_WIDTH // ATT_Q_HEADS
ATT_KV_WIDTH = ATT_KV_HEADS * ATT_HEAD_DIM
Q_BLOCK = 128
GRID_W = 64
ROPE_THETA = 10000.0
EPS = 1e-6
IN_SIZES = (RET_WIDTH, RET_WIDTH, RET_WIDTH, RET_WIDTH, ATT_WIDTH, ATT_KV_WIDTH, ATT_KV_WIDTH, ATT_WIDTH)
IN_COLS = sum(IN_SIZES)

kernel_name = "hybrid_retention_gqa_encoder"


def rms_norm(x, g):
    xf = x.astype(jnp.float32)
    y = xf * lax.rsqrt(jnp.mean(xf * xf, axis=-1, keepdims=True) + EPS)
    return (y * g.astype(jnp.float32)).astype(x.dtype)


def grid_angles(seq_len, head_dim):
    rows = seq_len // GRID_W
    row = jnp.repeat(jnp.arange(rows, dtype=jnp.float32), GRID_W)
    col = jnp.tile(jnp.arange(GRID_W, dtype=jnp.float32), rows)
    half = head_dim // 2
    freqs = ROPE_THETA ** (-jnp.arange(0, half, 2, dtype=jnp.float32) / half)
    return row[:, None] * freqs[None, :], col[:, None] * freqs[None, :]


def rotate(x, ang):
    cos = jnp.cos(ang)[None, :, None, :].astype(x.dtype)
    sin = jnp.sin(ang)[None, :, None, :].astype(x.dtype)
    x1, x2 = jnp.split(x, 2, axis=-1)
    return jnp.concatenate([x1 * cos - x2 * sin, x2 * cos + x1 * sin], axis=-1)


def axial_rope(x, ang_row, ang_col):
    xr, xc = jnp.split(x, 2, axis=-1)
    return jnp.concatenate([rotate(xr, ang_row), rotate(xc, ang_col)], axis=-1)


def retention_one_direction(q, k, v, log_gamma, include_diag):
    b, s, h, dk = q.shape
    dv = v.shape[-1]
    n = s // RET_CHUNK
    qc = q.reshape(b, n, RET_CHUNK, h, dk)
    kc = k.reshape(b, n, RET_CHUNK, h, dk)
    vc = v.reshape(b, n, RET_CHUNK, h, dv)
    pos = jnp.arange(RET_CHUNK, dtype=jnp.float32)
    diff = pos[:, None] - pos[None, :]
    mask = (diff >= 0) if include_diag else (diff > 0)
    dmat = jnp.where(mask[None], jnp.exp(log_gamma[:, None, None] * jnp.maximum(diff, 0.0)[None]), 0.0)
    scores = jnp.einsum('bnchd,bnmhd->bnhcm', qc, kc) * dmat
    intra = jnp.einsum('bnhcm,bnmhe->bnche', scores, vc)
    k_dec = jnp.exp((RET_CHUNK - 1 - pos)[:, None] * log_gamma[None, :])
    states = jnp.einsum('bnchd,bnche->bnhde', kc * k_dec[:, :, None], vc)
    chunk_decay = jnp.exp(RET_CHUNK * log_gamma)[:, None, None]

    def step(r, st):
        return r * chunk_decay + st, r

    _, r_prev = lax.scan(step, jnp.zeros_like(states[:, 0]), jnp.moveaxis(states, 1, 0))
    r_prev = jnp.moveaxis(r_prev, 0, 1)
    q_dec = jnp.exp((pos + 1.0)[:, None] * log_gamma[None, :])
    cross = jnp.einsum('bnchd,bnhde->bnche', qc * q_dec[:, :, None], r_prev)
    return (intra + cross).reshape(b, s, h, dv)


def bidirectional_retention(q, k, v, log_rate_fwd, log_rate_bwd, gn_g):
    dtype = v.dtype
    qf, kf, vf = q.astype(jnp.float32), k.astype(jnp.float32) * (RET_HEAD_DIM ** -0.5), v.astype(jnp.float32)
    lg_f = -jnp.exp(log_rate_fwd.astype(jnp.float32))
    lg_b = -jnp.exp(log_rate_bwd.astype(jnp.float32))
    fwd = retention_one_direction(qf, kf, vf, lg_f, True)
    bwd = jnp.flip(retention_one_direction(jnp.flip(qf, 1), jnp.flip(kf, 1), jnp.flip(vf, 1), lg_b, False), 1)
    o = fwd + bwd
    o = o * lax.rsqrt(jnp.mean(o * o, axis=-1, keepdims=True) + EPS) * gn_g.astype(jnp.float32)
    b, s = o.shape[:2]
    return o.reshape(b, s, RET_WIDTH).astype(dtype)


def blocked_gqa(q, k, v):
    b, s, _, d = q.shape
    nb = s // Q_BLOCK
    qb = jnp.moveaxis(q.reshape(b, nb, Q_BLOCK, ATT_KV_HEADS, ATT_GROUP, d), 1, 0)
    scale = 1.0 / math.sqrt(d)

    def one_block(qi):
        sc = jnp.einsum('bqkgd,bskd->bkgqs', qi, k).astype(jnp.float32) * scale
        p = jax.nn.softmax(sc, axis=-1).astype(v.dtype)
        return jnp.einsum('bkgqs,bskd->bqkgd', p, v)

    out = lax.map(one_block, qb)
    return jnp.moveaxis(out, 0, 1).reshape(b, s, ATT_WIDTH)


def hybrid_layer(x, c, norm_g, w_ada, b_ada, w_in, log_rate_fwd, log_rate_bwd, gn_g, q_norm_g, k_norm_g, w_out):
    b, s, _ = x.shape
    mod = jax.nn.silu(c) @ w_ada + b_ada
    shift, scale, gate = jnp.split(mod, 3, axis=-1)
    h = rms_norm(x, norm_g) * (1.0 + scale[:, None, :]) + shift[:, None, :]
    proj = h @ w_in
    splits = list(np.cumsum(IN_SIZES)[:-1])
    rq, rk, rv, rg, aq, ak, av, ag = jnp.split(proj, splits, axis=-1)

    ang_r_ret, ang_c_ret = grid_angles(s, RET_HEAD_DIM)
    ang_r_att, ang_c_att = grid_angles(s, ATT_HEAD_DIM)

    rq = axial_rope(rq.reshape(b, s, RET_HEADS, RET_HEAD_DIM), ang_r_ret, ang_c_ret)
    rk = axial_rope(rk.reshape(b, s, RET_HEADS, RET_HEAD_DIM), ang_r_ret, ang_c_ret)
    rv = rv.reshape(b, s, RET_HEADS, RET_HEAD_DIM)
    ret_out = bidirectional_retention(rq, rk, rv, log_rate_fwd, log_rate_bwd, gn_g)

    aq = axial_rope(rms_norm(aq.reshape(b, s, ATT_Q_HEADS, ATT_HEAD_DIM), q_norm_g), ang_r_att, ang_c_att)
    ak = axial_rope(rms_norm(ak.reshape(b, s, ATT_KV_HEADS, ATT_HEAD_DIM), k_norm_g), ang_r_att, ang_c_att)
    av = av.reshape(b, s, ATT_KV_HEADS, ATT_HEAD_DIM)
    att_out = blocked_gqa(aq, ak, av)

    mixed = jnp.concatenate([ret_out * jax.nn.silu(rg), att_out * jax.nn.silu(ag)], axis=-1) @ w_out
    return x + gate[:, None, :] * mixed


def trunk(x, c, norm_g, w_ada, b_ada, w_in, ret_log_rate_fwd, ret_log_rate_bwd, ret_gn_g, q_norm_g, k_norm_g, w_out):
    for l in range(DEPTH):
        x = hybrid_layer(x, c, norm_g[l], w_ada[l], b_ada[l], w_in[l], ret_log_rate_fwd[l], ret_log_rate_bwd[l],
                         ret_gn_g[l], q_norm_g[l], k_norm_g[l], w_out[l])
    return x


def setup_inputs(seed: int = 0) -> dict:
    key = jax.random.key(seed)
    ks = jax.random.split(key, 15)
    f32 = jnp.float32
    base_rate = jnp.log(-jnp.log1p(-(2.0 ** (-5.0 - jnp.arange(RET_HEADS, dtype=f32)))))
    return {
        "x_prompt": jax.random.normal(ks[0], (BATCH, SEQ, D_MODEL), f32),
        "x_sample": jax.random.normal(ks[1], (DEC_BATCH, DEC_SEQ, D_MODEL), f32),
        "c_prompt": jax.random.normal(ks[2], (BATCH, D_MODEL), f32),
        "c_sample": jax.random.normal(ks[3], (DEC_BATCH, D_MODEL), f32),
        "norm_g": 1.0 + 0.02 * jax.random.normal(ks[4], (DEPTH, D_MODEL), f32),
        "w_ada": 0.5 * D_MODEL ** -0.5 * jax.random.normal(ks[5], (DEPTH, D_MODEL, 3 * D_MODEL), f32),
        "b_ada": 0.02 * jax.random.normal(ks[6], (DEPTH, 3 * D_MODEL), f32),
        "w_in": D_MODEL ** -0.5 * jax.random.normal(ks[7], (DEPTH, D_MODEL, IN_COLS), f32),
        "ret_log_rate_fwd": base_rate[None, :] + 0.05 * jax.random.normal(ks[8], (DEPTH, RET_HEADS), f32),
        "ret_log_rate_bwd": base_rate[None, :] + 0.05 * jax.random.normal(ks[9], (DEPTH, RET_HEADS), f32),
        "ret_gn_g": 1.0 + 0.02 * jax.random.normal(ks[10], (DEPTH, RET_HEADS, RET_HEAD_DIM), f32),
        "q_norm_g": 1.0 + 0.02 * jax.random.normal(ks[11], (DEPTH, ATT_HEAD_DIM), f32),
        "k_norm_g": 1.0 + 0.02 * jax.random.normal(ks[12], (DEPTH, ATT_HEAD_DIM), f32),
        "w_out": MIX_WIDTH ** -0.5 * jax.random.normal(ks[13], (DEPTH, MIX_WIDTH, D_MODEL), f32),
    }


def reference(x_prompt, x_sample, c_prompt, c_sample, norm_g, w_ada, b_ada, w_in, ret_log_rate_fwd, ret_log_rate_bwd,
              ret_gn_g, q_norm_g, k_norm_g, w_out):
    y_prompt = trunk(x_prompt, c_prompt, norm_g, w_ada, b_ada, w_in, ret_log_rate_fwd, ret_log_rate_bwd, ret_gn_g,
                     q_norm_g, k_norm_g, w_out)
    y_sample = trunk(x_sample, c_sample, norm_g, w_ada, b_ada, w_in, ret_log_rate_fwd, ret_log_rate_bwd, ret_gn_g,
                     q_norm_g, k_norm_g, w_out)
    return (y_prompt, y_sample)
```

```python
import functools
import math

import jax
import jax.numpy as jnp
from jax import lax
from jax.experimental import pallas as pl
from jax.experimental.pallas import tpu as pltpu

F32 = jnp.float32
BF16 = jnp.bfloat16

D_MODEL = 1024
RET_HEADS = 4
RET_HEAD_DIM = 128
RET_WIDTH = RET_HEADS * RET_HEAD_DIM
RET_CHUNK = 128
ATT_Q_HEADS = 8
ATT_KV_HEADS = 2
ATT_GROUP = ATT_Q_HEADS // ATT_KV_HEADS
ATT_HEAD_DIM = 64
ATT_WIDTH = ATT_Q_HEADS * ATT_HEAD_DIM
ATT_KV_WIDTH = ATT_KV_HEADS * ATT_HEAD_DIM
GRID_W = 64
ROPE_THETA = 10000.0
EPS = 1e-6

_RQ, _RK, _RV, _RG = 0, RET_WIDTH, 2 * RET_WIDTH, 3 * RET_WIDTH
_AQ = 4 * RET_WIDTH
_AK = _AQ + ATT_WIDTH
_AV = _AK + ATT_KV_WIDTH
_AG = _AV + ATT_KV_WIDTH
IN_COLS = _AG + ATT_WIDTH

LANES = 128
TOKEN_TILE = 512
KEY_TILE = 512
QUERY_TILE = 256
VMEM_LIMIT = 48 * 1024 * 1024


def _silu(v):
    return v * (1.0 / (1.0 + jnp.exp(-v)))


def _adaln_kernel(c_ref, w_ref, b_ref, o_ref):
    a = _silu(c_ref[...]).astype(BF16)
    o_ref[...] = jnp.dot(a, w_ref[...].astype(BF16), preferred_element_type=F32) + b_ref[...]


def _adaln(c, w_ada, b_ada):
    n = c.shape[0]
    tn = D_MODEL
    return pl.pallas_call(
        _adaln_kernel,
        grid=(3 * D_MODEL // tn,),
        in_specs=[
            pl.BlockSpec((n, D_MODEL), lambda j: (0, 0)),
            pl.BlockSpec((D_MODEL, tn), lambda j: (0, j)),
            pl.BlockSpec((1, tn), lambda j: (0, j)),
        ],
        out_specs=pl.BlockSpec((n, tn), lambda j: (0, j)),
        out_shape=jax.ShapeDtypeStruct((n, 3 * D_MODEL), F32),
        compiler_params=pltpu.CompilerParams(
            dimension_semantics=("arbitrary",), vmem_limit_bytes=VMEM_LIMIT),
        name="adaln",
    )(c, w_ada, b_ada)


def _rope_tables(seq_len):
    t = jnp.arange(seq_len)
    row = (t // GRID_W).astype(F32)
    col = (t % GRID_W).astype(F32)

    def one(head_dim, reps):
        half = head_dim // 2
        freqs = ROPE_THETA ** (-jnp.arange(0, half, 2, dtype=F32) / half)
        ar, ac = row[:, None] * freqs[None, :], col[:, None] * freqs[None, :]
        z = jnp.zeros_like(ar)
        cos = jnp.concatenate([jnp.cos(ar), jnp.cos(ar), jnp.cos(ac), jnp.cos(ac)], axis=-1)
        sl = jnp.concatenate([-jnp.sin(ar), z, -jnp.sin(ac), z], axis=-1)
        sr = jnp.concatenate([z, jnp.sin(ar), z, jnp.sin(ac)], axis=-1)
        return [jnp.tile(a, (1, reps)) for a in (cos, sl, sr)]

    return jnp.concatenate(one(RET_HEAD_DIM, 1) + one(ATT_HEAD_DIM, 2), axis=-1)


def _rope(xb, cos, sl, sr, quarter):
    return (xb * cos + pltpu.roll(xb, LANES - quarter, 1) * sl
            + pltpu.roll(xb, quarter, 1) * sr)


def _inproj_kernel(x_ref, mod_ref, ng_ref, w_ref, tab_ref, qg_ref, kg_ref, seg_ref,
                   rq_ref, rk_ref, rv_ref, rg_ref, aqt_ref, ak_ref, avt_ref, ag_ref):
    x = x_ref[0]
    ms = jnp.mean(x * x, axis=-1, keepdims=True)
    y = x * lax.rsqrt(ms + EPS) * ng_ref[...]
    shift = mod_ref[0, :, 0:D_MODEL]
    scale = mod_ref[0, :, D_MODEL:2 * D_MODEL]
    hb = (y * (1.0 + scale) + shift).astype(BF16)

    def proj(a, b):
        return jnp.dot(hb, w_ref[:, a:b], preferred_element_type=F32)

    rcos, rsl, rsr = (tab_ref[:, i * LANES:(i + 1) * LANES] for i in range(3))
    acos, asl, asr = (tab_ref[:, i * LANES:(i + 1) * LANES] for i in range(3, 6))
    rq4, rk4 = RET_HEAD_DIM // 4, ATT_HEAD_DIM // 4

    p = proj(_RQ, _RQ + RET_WIDTH)
    for h in range(RET_HEADS):
        sl = slice(h * LANES, (h + 1) * LANES)
        rq_ref[0, :, sl] = _rope(p[:, sl], rcos, rsl, rsr, rq4).astype(BF16)
    p = proj(_RK, _RK + RET_WIDTH)
    for h in range(RET_HEADS):
        sl = slice(h * LANES, (h + 1) * LANES)
        rk_ref[0, :, sl] = (_rope(p[:, sl], rcos, rsl, rsr, rq4)
                            * (RET_HEAD_DIM ** -0.5)).astype(BF16)
    rv_ref[0] = proj(_RV, _RV + RET_WIDTH).astype(BF16)
    rg_ref[0] = _silu(proj(_RG, _RG + RET_WIDTH)).astype(BF16)

    def qk_norm(xb, gain):
        ms = jnp.dot((xb * xb).astype(BF16), seg_ref[...], preferred_element_type=F32)
        return xb * lax.rsqrt(ms + EPS) * gain

    p = proj(_AQ, _AQ + ATT_WIDTH)
    for j in range(ATT_WIDTH // LANES):
        sl = slice(j * LANES, (j + 1) * LANES)
        q = _rope(qk_norm(p[:, sl], qg_ref[...]), acos, asl, asr, rk4)
        aqt_ref[0, sl, :] = (q * (ATT_HEAD_DIM ** -0.5)).T.astype(BF16)
    p = proj(_AK, _AK + ATT_KV_WIDTH)
    k = _rope(qk_norm(p, kg_ref[...]), acos, asl, asr, rk4).astype(BF16)
    for g in range(ATT_KV_HEADS):
        ak_ref[0, g] = k[:, g * ATT_HEAD_DIM:(g + 1) * ATT_HEAD_DIM]
    avt_ref[0, 0] = proj(_AV, _AV + ATT_KV_WIDTH).T.astype(BF16)
    ag_ref[0] = _silu(proj(_AG, _AG + ATT_WIDTH)).astype(BF16)


def _inproj(x, mod, norm_g, w_in_bf, tab, qg, kg, seg):
    b, s, _ = x.shape
    tm = TOKEN_TILE
    nt = s // tm
    tok = lambda w: pl.BlockSpec((1, tm, w), lambda bi, si: (bi, si, 0))
    const = lambda shape: pl.BlockSpec(shape, lambda bi, si: (0,) * len(shape))
    out_shape = [
        jax.ShapeDtypeStruct((b, s, RET_WIDTH), BF16),
        jax.ShapeDtypeStruct((b, s, RET_WIDTH), BF16),
        jax.ShapeDtypeStruct((b, s, RET_WIDTH), BF16),
        jax.ShapeDtypeStruct((b, s, RET_WIDTH), BF16),
        jax.ShapeDtypeStruct((b, ATT_WIDTH, s), BF16),
        jax.ShapeDtypeStruct((b, ATT_KV_HEADS, s, ATT_HEAD_DIM), BF16),
        jax.ShapeDtypeStruct((b, nt, ATT_KV_WIDTH, tm), BF16),
        jax.ShapeDtypeStruct((b, s, ATT_WIDTH), BF16),
    ]
    out_specs = [
        tok(RET_WIDTH), tok(RET_WIDTH), tok(RET_WIDTH), tok(RET_WIDTH),
        pl.BlockSpec((1, ATT_WIDTH, tm), lambda bi, si: (bi, 0, si)),
        pl.BlockSpec((1, ATT_KV_HEADS, tm, ATT_HEAD_DIM), lambda bi, si: (bi, 0, si, 0)),
        pl.BlockSpec((1, 1, ATT_KV_WIDTH, tm), lambda bi, si: (bi, si, 0, 0)),
        tok(ATT_WIDTH),
    ]
    return pl.pallas_call(
        _inproj_kernel,
        grid=(b, nt),
        in_specs=[
            tok(D_MODEL),
            pl.BlockSpec((1, 1, 3 * D_MODEL), lambda bi, si: (bi, 0, 0)),
            const((1, D_MODEL)),
            const((D_MODEL, IN_COLS)),
            pl.BlockSpec((tm, 6 * LANES), lambda bi, si: (si, 0)),
            const((1, LANES)), const((1, LANES)), const((LANES, LANES)),
        ],
        out_specs=out_specs,
        out_shape=out_shape,
        compiler_params=pltpu.CompilerParams(
            dimension_semantics=("arbitrary", "arbitrary"), vmem_limit_bytes=VMEM_LIMIT),
        name="inproj",
    )(x, mod, norm_g, w_in_bf, tab, qg, kg, seg)


def _retention_kernel(q_ref, k_ref, v_ref, g_ref, lrf_ref, lrb_ref, gn_ref, o_ref, rb_scr):
    c = RET_CHUNK
    n_chunks = q_ref.shape[1] // c
    lgf = -jnp.exp(lrf_ref[0, 0:1, :])
    lgb = -jnp.exp(lrb_ref[0, 0:1, :])
    r = lax.broadcasted_iota(jnp.int32, (c, c), 0).astype(F32)
    m = lax.broadcasted_iota(jnp.int32, (c, c), 1).astype(F32)
    diff = r - m
    dmat = jnp.where(diff >= 0, jnp.exp(lgf * jnp.maximum(diff, 0.0)),
                     jnp.exp(lgb * jnp.maximum(-diff, 0.0)))
    kdec_f = jnp.exp(lgf * (c - 1.0 - r))
    kdec_b = jnp.exp(lgb * r)
    qdec_f = jnp.exp(lgf * (r + 1.0))
    qdec_b = jnp.exp(lgb * (c - r))
    cd_f = jnp.exp(c * lgf)
    cd_b = jnp.exp(c * lgb)
    gn = gn_ref[0]

    def chunk(ref, n):
        return ref[0, pl.ds(pl.multiple_of(n * c, c), c), :]

    def state(kn, vn, dec):
        kd = (kn.astype(F32) * dec).astype(BF16)
        return lax.dot_general(kd, vn, (((0,), (0,)), ((), ())), preferred_element_type=F32)

    def back(i, rb):
        n = n_chunks - 1 - i
        rb_scr[n] = rb.astype(BF16)
        return rb * cd_b + state(chunk(k_ref, n), chunk(v_ref, n), kdec_b)

    lax.fori_loop(0, n_chunks, back, jnp.zeros((c, c), F32))

    def fwd(n, rf):
        qn, kn, vn = chunk(q_ref, n), chunk(k_ref, n), chunk(v_ref, n)
        s = lax.dot_general(qn, kn, (((1,), (1,)), ((), ())), preferred_element_type=F32)
        qf = qn.astype(F32)
        o = (jnp.dot((s * dmat).astype(BF16), vn, preferred_element_type=F32)
             + jnp.dot((qf * qdec_f).astype(BF16), rf.astype(BF16), preferred_element_type=F32)
             + jnp.dot((qf * qdec_b).astype(BF16), rb_scr[n], preferred_element_type=F32))
        o = o * lax.rsqrt(jnp.mean(o * o, axis=-1, keepdims=True) + EPS) * gn
        o_ref[0, pl.ds(pl.multiple_of(n * c, c), c), :] = (
            o * chunk(g_ref, n).astype(F32)).astype(BF16)
        return rf * cd_f + state(kn, vn, kdec_f)

    lax.fori_loop(0, n_chunks, fwd, jnp.zeros((c, c), F32))


def _retention(rq, rk, rv, rg, lrf, lrb, gn):
    b, s, _ = rq.shape
    seq = pl.BlockSpec((1, s, RET_HEAD_DIM), lambda bi, h: (bi, 0, h))
    rate = pl.BlockSpec((1, 8, LANES), lambda bi, h: (h, 0, 0))
    return pl.pallas_call(
        _retention_kernel,
        grid=(b, RET_HEADS),
        in_specs=[seq, seq, seq, seq, rate, rate,
                  pl.BlockSpec((1, 1, RET_HEAD_DIM), lambda bi, h: (h, 0, 0))],
        out_specs=seq,
        out_shape=jax.ShapeDtypeStruct((b, s, RET_WIDTH), BF16),
        scratch_shapes=[pltpu.VMEM((s // RET_CHUNK, RET_CHUNK, RET_HEAD_DIM), BF16)],
        compiler_params=pltpu.CompilerParams(
            dimension_semantics=("arbitrary", "arbitrary"), vmem_limit_bytes=VMEM_LIMIT),
        name="retention",
    )(rq, rk, rv, rg, lrf, lrb, gn)


def _attention_kernel(qt_ref, k_ref, vt_ref, g_ref, o_ref):
    tq = qt_ref.shape[2]
    n_kb = vt_ref.shape[1]
    tk = vt_ref.shape[3]
    d = ATT_HEAD_DIM
    outs = []
    for h in range(ATT_GROUP):
        qt = qt_ref[0, h * d:(h + 1) * d, :]

        def body(kb, carry, qt=qt):
            m, l, acc = carry
            kblk = k_ref[0, 0, pl.ds(pl.multiple_of(kb * tk, tk), tk), :]
            s = jnp.dot(kblk, qt, preferred_element_type=F32)
            m_new = jnp.maximum(m, jnp.max(s, axis=0, keepdims=True))
            alpha = jnp.exp(m - m_new)
            p = jnp.exp(s - m_new)
            l = alpha * l + jnp.sum(p, axis=0, keepdims=True)
            acc = alpha * acc + jnp.dot(vt_ref[0, kb], p.astype(BF16),
                                        preferred_element_type=F32)
            return m_new, l, acc

        init = (jnp.full((1, tq), -1e30, F32), jnp.zeros((1, tq), F32), jnp.zeros((d, tq), F32))
        _, l, acc = lax.fori_loop(0, n_kb, body, init)
        outs.append(acc / l)
    out = jnp.concatenate(outs, axis=0).T
    o_ref[0] = (out * g_ref[0].astype(F32)).astype(BF16)


def _attention(aqt, ak, avt, ag):
    b, _, s = aqt.shape
    tq = QUERY_TILE
    n_kb, tk = avt.shape[1], avt.shape[3]
    gw = ATT_GROUP * ATT_HEAD_DIM
    return pl.pallas_call(
        _attention_kernel,
        grid=(b, ATT_KV_HEADS, s // tq),
        in_specs=[
            pl.BlockSpec((1, gw, tq), lambda bi, g, qi: (bi, g, qi)),
            pl.BlockSpec((1, 1, s, ATT_HEAD_DIM), lambda bi, g, qi: (bi, g, 0, 0)),
            pl.BlockSpec((1, n_kb, ATT_HEAD_DIM, tk), lambda bi, g, qi: (bi, 0, g, 0)),
            pl.BlockSpec((1, tq, gw), lambda bi, g, qi: (bi, qi, g)),
        ],
        out_specs=pl.BlockSpec((1, tq, gw), lambda bi, g, qi: (bi, qi, g)),
        out_shape=jax.ShapeDtypeStruct((b, s, ATT_WIDTH), BF16),
        compiler_params=pltpu.CompilerParams(
            dimension_semantics=("arbitrary", "arbitrary", "arbitrary"),
            vmem_limit_bytes=VMEM_LIMIT),
        name="attention",
    )(aqt, ak, avt, ag)


def _outproj_kernel(x_ref, mod_ref, r_ref, a_ref, w_ref, o_ref):
    mixed = (jnp.dot(r_ref[0], w_ref[0:RET_WIDTH, :], preferred_element_type=F32)
             + jnp.dot(a_ref[0], w_ref[RET_WIDTH:, :], preferred_element_type=F32))
    gate = mod_ref[0, :, 2 * D_MODEL:3 * D_MODEL]
    o_ref[0] = x_ref[0] + gate * mixed


def _outproj(x, mod, ret, att, w_out_bf):
    b, s, _ = x.shape
    tm = TOKEN_TILE
    tok = lambda w: pl.BlockSpec((1, tm, w), lambda bi, si: (bi, si, 0))
    return pl.pallas_call(
        _outproj_kernel,
        grid=(b, s // tm),
        in_specs=[
            tok(D_MODEL),
            pl.BlockSpec((1, 1, 3 * D_MODEL), lambda bi, si: (bi, 0, 0)),
            tok(RET_WIDTH), tok(ATT_WIDTH),
            pl.BlockSpec((RET_WIDTH + ATT_WIDTH, D_MODEL), lambda bi, si: (0, 0)),
        ],
        out_specs=tok(D_MODEL),
        out_shape=jax.ShapeDtypeStruct((b, s, D_MODEL), F32),
        compiler_params=pltpu.CompilerParams(
            dimension_semantics=("arbitrary", "arbitrary"), vmem_limit_bytes=VMEM_LIMIT),
        name="outproj",
    )(x, mod, ret, att, w_out_bf)


def _layer(x, mod, tab, norm_g, w_in_bf, lrf, lrb, gn, qg, kg, seg, w_out_bf):
    s = x.shape[1]
    rq, rk, rv, rg, aqt, ak, avt, ag = _inproj(x, mod, norm_g, w_in_bf, tab[:s], qg, kg, seg)
    ret = _retention(rq, rk, rv, rg, lrf, lrb, gn)
    att = _attention(aqt, ak, avt, ag)
    return _outproj(x, mod, ret, att, w_out_bf)


def kernel(x_prompt, x_sample, c_prompt, c_sample, norm_g, w_ada, b_ada, w_in, ret_log_rate_fwd,
           ret_log_rate_bwd, ret_gn_g, q_norm_g, k_norm_g, w_out):
    depth = norm_g.shape[0]
    nb = x_prompt.shape[0]
    tab = _rope_tables(max(x_prompt.shape[1], x_sample.shape[1]))
    head_id = jnp.arange(LANES) // ATT_HEAD_DIM
    seg = jnp.where(head_id[:, None] == head_id[None, :], 1.0 / ATT_HEAD_DIM, 0.0).astype(BF16)
    xs = [x_prompt, x_sample]
    c = jnp.concatenate([c_prompt, c_sample], axis=0)
    for l in range(depth):
        mod = _adaln(c, w_ada[l], b_ada[l][None, :])[:, None, :]
        mods = [mod[:nb], mod[nb:]]
        rate = lambda a: jnp.broadcast_to(a[:, None, None], (RET_HEADS, 8, LANES))
        args = (norm_g[l][None, :], w_in[l].astype(BF16), rate(ret_log_rate_fwd[l]),
                rate(ret_log_rate_bwd[l]), ret_gn_g[l][:, None, :],
                jnp.tile(q_norm_g[l], LANES // ATT_HEAD_DIM)[None, :],
                jnp.tile(k_norm_g[l], LANES // ATT_HEAD_DIM)[None, :], seg, w_out[l].astype(BF16))
        xs = [_layer(x, m, tab, *args) for x, m in zip(xs, mods)]
    return tuple(xs)
```
